```python
import math
import jax, jax.numpy as jnp
from jax import lax
import numpy as np

D_MODEL = 1024
BATCH = 4
SEQ = 4096
DEPTH = 4

GRID_W = 64
CTX_LEN = 256
W_BR = 512
CONV_K = 31
CHUNK = 128
SGU_HEADS = 8
SSM_GROUP = 16
SSM_GROUPS = W_BR // SSM_GROUP
SSM_STATE = 64
N_DIR = 2
D_FF = (8 * D_MODEL + 3 * 256 - 1) // (3 * 256) * 256
N_IN = 5 * W_BR + 3 * D_MODEL
EPS = 1e-6

kernel_name = "hybrid_conv_gmlp_s5_dit_block"


def _rmsnorm(x, g):
    x32 = x.astype(jnp.float32)
    y = x32 * lax.rsqrt(jnp.mean(x32 * x32, axis=-1, keepdims=True) + EPS)
    return y.astype(x.dtype) * g


def _layernorm(x, g, b):
    x32 = x.astype(jnp.float32)
    mu = jnp.mean(x32, axis=-1, keepdims=True)
    var = jnp.mean(jnp.square(x32 - mu), axis=-1, keepdims=True)
    return ((x32 - mu) * lax.rsqrt(var + EPS)).astype(x.dtype) * g + b


def _sincos_1d(pos, dim):
    half = dim // 2
    omega = 1.0 / (10000.0 ** (jnp.arange(half, dtype=jnp.float32) / half))
    ang = pos[:, None] * omega[None, :]
    return jnp.concatenate([jnp.sin(ang), jnp.cos(ang)], axis=-1)


def _grid_pos_embed(rows, dim):
    emb_r = _sincos_1d(jnp.arange(rows, dtype=jnp.float32), dim // 2)
    emb_c = _sincos_1d(jnp.arange(GRID_W, dtype=jnp.float32), dim // 2)
    pe = jnp.concatenate([
        jnp.broadcast_to(emb_r[:, None, :], (rows, GRID_W, dim // 2)),
        jnp.broadcast_to(emb_c[None, :, :], (rows, GRID_W, dim // 2))], axis=-1)
    return pe.reshape(rows * GRID_W, dim)


def _modulate(h, shift, scale):
    return h * (1.0 + scale) + shift


def _dwconv(x, w, b):
    y = lax.conv_general_dilated(
        x, w[:, None, :], window_strides=(1,), padding=[(CONV_K // 2, CONV_K // 2)],
        dimension_numbers=('NWC', 'WIO', 'NWC'), feature_group_count=x.shape[-1])
    return y + b


def _chunk_spatial(v, w_s, b_s):
    n_b, n_l, n_c = v.shape
    v = v.reshape(n_b, n_l // CHUNK, CHUNK, SGU_HEADS, n_c // SGU_HEADS)
    y = jnp.einsum('hpq,bnqhc->bnphc', w_s, v) + b_s.T[None, None, :, :, None]
    return y.reshape(n_b, n_l, n_c)


def _complex_affine_combine(e1, e2):
    a1r, a1i, b1r, b1i = e1
    a2r, a2i, b2r, b2i = e2
    return (a1r * a2r - a1i * a2i,
            a1r * a2i + a1i * a2r,
            a2r * b1r - a2i * b1i + b2r,
            a2r * b1i + a2i * b1r + b2i)


def _ssm_states(u, h0, lam_re, lam_im, log_dt, b_re, b_im):
    lam_re, lam_im, log_dt, b_re, b_im = (t.astype(jnp.float32) for t in (lam_re, lam_im, log_dt, b_re, b_im))
    dt = jnp.exp(log_dt)[:, None]
    a_mag = jnp.exp(lam_re * dt)
    ang = lam_im * dt
    a_re, a_im = a_mag * jnp.cos(ang), a_mag * jnp.sin(ang)
    den = lam_re * lam_re + lam_im * lam_im
    f_re = ((a_re - 1.0) * lam_re + a_im * lam_im) / den
    f_im = (a_im * lam_re - (a_re - 1.0) * lam_im) / den
    bb_re = f_re[..., None] * b_re - f_im[..., None] * b_im
    bb_im = f_re[..., None] * b_im + f_im[..., None] * b_re
    n_b, n_l, _ = u.shape
    ug = u.reshape(n_b, n_l, SSM_GROUPS, SSM_GROUP)
    x_re = jnp.einsum('blgi,gpi->blgp', ug, bb_re)
    x_im = jnp.einsum('blgi,gpi->blgp', ug, bb_im)
    h0_re, h0_im = h0
    x_re = x_re.at[:, 0].add(a_re * h0_re - a_im * h0_im)
    x_im = x_im.at[:, 0].add(a_re * h0_im + a_im * h0_re)
    ar = jnp.broadcast_to(a_re, x_re.shape)
    ai = jnp.broadcast_to(a_im, x_im.shape)
    _, _, h_re, h_im = lax.associative_scan(_complex_affine_combine, (ar, ai, x_re, x_im), axis=1)
    return h_re, h_im


def _ssm_readout(h, c_re, c_im):
    h_re, h_im = h
    y = (jnp.einsum('blgp,gip->blgi', h_re, c_re.astype(jnp.float32))
         - jnp.einsum('blgp,gip->blgi', h_im, c_im.astype(jnp.float32)))
    return y.reshape(y.shape[0], y.shape[1], W_BR)


def _ssm_scan_pair(u, h0_fwd, h0_bwd, lp):
    h_f = _ssm_states(u, h0_fwd, lp['lam_re'][0], lp['lam_im'][0], lp['log_dt'][0], lp['b_re'][0], lp['b_im'][0])
    h_b = _ssm_states(u[:, ::-1], h0_bwd, lp['lam_re'][1], lp['lam_im'][1], lp['log_dt'][1], lp['b_re'][1], lp['b_im'][1])
    return h_f, h_b


def _final(h):
    return (h[0][:, -1], h[1][:, -1])


def _mix_stream(h, h0_fwd, h0_bwd, lp):
    z = h @ lp['w_in'] + lp['b_in']
    za, zb, zs, zg = jnp.split(z, [2 * W_BR, 4 * W_BR, 5 * W_BR], axis=-1)
    a = za[..., :W_BR] * jax.nn.sigmoid(za[..., W_BR:])
    a = _dwconv(a, lp['conv_w'], lp['conv_b'])
    a = jax.nn.silu(_layernorm(a, lp['conv_ln_g'], lp['conv_ln_b']))
    o_a = a @ lp['conv_w_out'] + lp['conv_b_out']
    zb = jax.nn.gelu(zb)
    u_b = zb[..., :W_BR]
    v_b = _layernorm(zb[..., W_BR:], lp['sgu_ln_g'], lp['sgu_ln_b'])
    o_b = (u_b * _chunk_spatial(v_b, lp['sgu_w'], lp['sgu_b'])) @ lp['sgu_w_out'] + lp['sgu_b_out']
    h_f, h_b = _ssm_scan_pair(zs.astype(jnp.float32), h0_fwd, h0_bwd, lp)
    y = (_ssm_readout(h_f, lp['c_re'][0], lp['c_im'][0])
         + _ssm_readout(h_b, lp['c_re'][1], lp['c_im'][1])[:, ::-1]).astype(zs.dtype)
    y = jax.nn.gelu(y + lp['ssm_d'] * zs)
    y = y * jax.nn.sigmoid(y @ lp['ssm_w_glu'] + lp['ssm_b_glu'])
    o_c = y @ lp['ssm_w_out'] + lp['ssm_b_out']
    g_a, g_b, g_c = jnp.split(jax.nn.sigmoid(zg), 3, axis=-1)
    out = (g_a * o_a + g_b * o_b + g_c * o_c) @ lp['w_o'] + lp['b_o']
    return out, h_f, h_b


def _swiglu(h, w_up, w_down):
    g, u = jnp.split(h @ w_up, 2, axis=-1)
    return (jax.nn.silu(g) * u) @ w_down


def setup_inputs(seed: int = 0) -> dict:
    key = jax.random.key(seed)
    ks = jax.random.split(key, 40)
    f32 = jnp.float32
    nrm = lambda k, shape, s: jax.random.normal(k, shape, f32) * s
    D, W, G, P, I = D_MODEL, W_BR, SSM_GROUPS, SSM_STATE, SSM_GROUP
    lam_im0 = jnp.pi * jnp.arange(P, dtype=f32)
    return {
        "x": nrm(ks[0], (BATCH, SEQ, D), 1.0),
        "c": nrm(ks[1], (BATCH, D), 1.0),
        "ctx": nrm(ks[2], (BATCH, CTX_LEN, D), 1.0),
        "c_ctx": nrm(ks[3], (D,), 1.0),
        "w_ada": nrm(ks[4], (DEPTH, D, 6 * D), 0.3 * D ** -0.5),
        "b_ada": nrm(ks[5], (DEPTH, 6 * D), 0.01),
        "norm_g": 1.0 + nrm(ks[6], (DEPTH, 4, D), 0.05),
        "w_in": nrm(ks[7], (DEPTH, D, N_IN), D ** -0.5),
        "b_in": nrm(ks[8], (DEPTH, N_IN), 0.01),
        "conv_w": nrm(ks[9], (DEPTH, CONV_K, W), CONV_K ** -0.5),
        "conv_b": nrm(ks[10], (DEPTH, W), 0.01),
        "conv_ln_g": 1.0 + nrm(ks[11], (DEPTH, W), 0.05),
        "conv_ln_b": nrm(ks[12], (DEPTH, W), 0.01),
        "conv_w_out": nrm(ks[13], (DEPTH, W, D), W ** -0.5),
        "conv_b_out": nrm(ks[14], (DEPTH, D), 0.01),
        "sgu_ln_g": 1.0 + nrm(ks[15], (DEPTH, W), 0.05),
        "sgu_ln_b": nrm(ks[16], (DEPTH, W), 0.01),
        "sgu_w": nrm(ks[17], (DEPTH, SGU_HEADS, CHUNK, CHUNK), 0.5 * CHUNK ** -0.5),
        "sgu_b": 1.0 + nrm(ks[18], (DEPTH, SGU_HEADS, CHUNK), 0.01),
        "sgu_w_out": nrm(ks[19], (DEPTH, W, D), W ** -0.5),
        "sgu_b_out": nrm(ks[20], (DEPTH, D), 0.01),
        "ssm_lam_re": -0.5 + nrm(ks[21], (DEPTH, N_DIR, G, P), 0.01),
        "ssm_lam_im": lam_im0 + nrm(ks[22], (DEPTH, N_DIR, G, P), 0.01),
        "ssm_log_dt": jax.random.uniform(ks[23], (DEPTH, N_DIR, G), f32, math.log(0.001), math.log(0.1)),
        "ssm_b_re": nrm(ks[24], (DEPTH, N_DIR, G, P, I), (2 * I) ** -0.5),
        "ssm_b_im": nrm(ks[25], (DEPTH, N_DIR, G, P, I), (2 * I) ** -0.5),
        "ssm_c_re": nrm(ks[26], (DEPTH, N_DIR, G, I, P), (2 * P) ** -0.5),
        "ssm_c_im": nrm(ks[27], (DEPTH, N_DIR, G, I, P), (2 * P) ** -0.5),
        "ssm_d": nrm(ks[28], (DEPTH, W), 1.0),
        "ssm_w_glu": nrm(ks[29], (DEPTH, W, W), W ** -0.5),
        "ssm_b_glu": nrm(ks[30], (DEPTH, W), 0.01),
        "ssm_w_out": nrm(ks[31], (DEPTH, W, D), W ** -0.5),
        "ssm_b_out": nrm(ks[32], (DEPTH, D), 0.01),
        "w_o": nrm(ks[33], (DEPTH, D, D), D ** -0.5),
        "b_o": nrm(ks[34], (DEPTH, D), 0.01),
        "ffn_w_up": nrm(ks[35], (DEPTH, D, 2 * D_FF), D ** -0.5),
        "ffn_w_down": nrm(ks[36], (DEPTH, D_FF, D), D_FF ** -0.5),
    }


def reference(x, c, ctx, c_ctx, w_ada, b_ada, norm_g, w_in, b_in, conv_w, conv_b, conv_ln_g, conv_ln_b,
              conv_w_out, conv_b_out, sgu_ln_g, sgu_ln_b, sgu_w, sgu_b, sgu_w_out, sgu_b_out,
              ssm_lam_re, ssm_lam_im, ssm_log_dt, ssm_b_re, ssm_b_im, ssm_c_re, ssm_c_im, ssm_d,
              ssm_w_glu, ssm_b_glu, ssm_w_out, ssm_b_out, w_o, b_o, ffn_w_up, ffn_w_down):
    n_b, n_l, _ = x.shape
    ROWS = n_l // GRID_W
    x = x + _grid_pos_embed(ROWS, D_MODEL).astype(x.dtype)[None]
    zero_state = (jnp.zeros((n_b, SSM_GROUPS, SSM_STATE), jnp.float32),
                  jnp.zeros((n_b, SSM_GROUPS, SSM_STATE), jnp.float32))
    s_c = jax.nn.silu(c)
    s_cc = jax.nn.silu(c_ctx)
    for l in range(DEPTH):
        last = l == DEPTH - 1
        lp = {
            'w_in': w_in[l], 'b_in': b_in[l], 'conv_w': conv_w[l], 'conv_b': conv_b[l],
            'conv_ln_g': conv_ln_g[l], 'conv_ln_b': conv_ln_b[l], 'conv_w_out': conv_w_out[l],
            'conv_b_out': conv_b_out[l], 'sgu_ln_g': sgu_ln_g[l], 'sgu_ln_b': sgu_ln_b[l],
            'sgu_w': sgu_w[l], 'sgu_b': sgu_b[l], 'sgu_w_out': sgu_w_out[l], 'sgu_b_out': sgu_b_out[l],
            'lam_re': ssm_lam_re[l], 'lam_im': ssm_lam_im[l], 'log_dt': ssm_log_dt[l],
            'b_re': ssm_b_re[l], 'b_im': ssm_b_im[l], 'c_re': ssm_c_re[l], 'c_im': ssm_c_im[l],
            'ssm_d': ssm_d[l], 'ssm_w_glu': ssm_w_glu[l], 'ssm_b_glu': ssm_b_glu[l],
            'ssm_w_out': ssm_w_out[l], 'ssm_b_out': ssm_b_out[l], 'w_o': w_o[l], 'b_o': b_o[l],
        }
        m_x = jnp.split((s_c @ w_ada[l] + b_ada[l])[:, None, :], 6, axis=-1)
        m_c = jnp.split((s_cc @ w_ada[l] + b_ada[l])[None, None, :], 6, axis=-1)

        hc = _modulate(_rmsnorm(ctx, norm_g[l, 0]), m_c[0], m_c[1])
        if last:
            u_c = (hc @ w_in[l][:, 4 * W_BR:5 * W_BR] + b_in[l][4 * W_BR:5 * W_BR]).astype(jnp.float32)
            hc_f, hc_b = _ssm_scan_pair(u_c, zero_state, zero_state, lp)
        else:
            out_c, hc_f, hc_b = _mix_stream(hc, zero_state, zero_state, lp)
            ctx = ctx + m_c[2] * _rmsnorm(out_c, norm_g[l, 1])
            hc2 = _modulate(_rmsnorm(ctx, norm_g[l, 2]), m_c[3], m_c[4])
            ctx = ctx + m_c[5] * _rmsnorm(_swiglu(hc2, ffn_w_up[l], ffn_w_down[l]), norm_g[l, 3])

        hx = _modulate(_rmsnorm(x, norm_g[l, 0]), m_x[0], m_x[1])
        out_x, _, _ = _mix_stream(hx, _final(hc_f), _final(hc_b), lp)
        x = x + m_x[2] * _rmsnorm(out_x, norm_g[l, 1])
        hx2 = _modulate(_rmsnorm(x, norm_g[l, 2]), m_x[3], m_x[4])
        x = x + m_x[5] * _rmsnorm(_swiglu(hx2, ffn_w_up[l], ffn_w_down[l]), norm_g[l, 3])
    return x
```

```python
import functools
import math

import jax
import jax.numpy as jnp
from jax import lax
from jax.experimental import pallas as pl
from jax.experimental.pallas import tpu as pltpu

EPS = 1e-6
GRID_W = 64
W_BR = 512
CONV_K = 31
CHUNK = 128
SGU_HEADS = 8
SSM_GROUP = 16
SSM_STATE = 64

LANES = 128
HALO = 16
SSM_Q = 8
SLAB_GROUPS = LANES // SSM_GROUP
SLAB_STATE = SLAB_GROUPS * SSM_STATE
FLAT = SSM_Q * LANES
VMEM_LIMIT = 56 * 1024 * 1024

BF16 = jnp.bfloat16
F32 = jnp.float32


def _cparams(sem):
    return pltpu.CompilerParams(dimension_semantics=sem, vmem_limit_bytes=VMEM_LIMIT)


def _const_spec(shape):
    nd = len(shape)
    return pl.BlockSpec(shape, lambda *_: (0,) * nd)


def _rms(x):
    return x * lax.rsqrt(jnp.mean(x * x, axis=-1, keepdims=True) + EPS)


def _prenorm(x, g, scale, shift):
    return (_rms(x) * g) * (1.0 + scale) + shift


def _layernorm(x, g, b):
    mu = jnp.mean(x, axis=-1, keepdims=True)
    xc = x - mu
    var = jnp.mean(xc * xc, axis=-1, keepdims=True)
    return (xc * lax.rsqrt(var + EPS)) * g + b


def _dot(a, b):
    return jnp.dot(a, b, preferred_element_type=F32)


def _mod_kernel(s_ref, w_ref, b_ref, o_ref):
    s = s_ref[...]
    s = s * jax.nn.sigmoid(s)
    o_ref[0, 0] = _dot(s, w_ref[0]) + b_ref[0, 0]


def _modulation(c, c_ctx, w_ada, b_ada):
    depth, d, _ = w_ada.shape
    n_b = c.shape[0]
    rows = -(-(n_b + 1) // 8) * 8
    s = jnp.zeros((rows, d), F32).at[:n_b].set(c).at[n_b].set(c_ctx)
    b3 = b_ada.reshape(depth, 6, 1, d)
    return pl.pallas_call(
        _mod_kernel,
        grid=(depth, 6),
        in_specs=[
            pl.BlockSpec((rows, d), lambda l, n: (0, 0)),
            pl.BlockSpec((1, d, d), lambda l, n: (l, 0, n)),
            pl.BlockSpec((1, 1, 1, d), lambda l, n: (l, n, 0, 0)),
        ],
        out_specs=pl.BlockSpec((1, 1, rows, d), lambda l, n: (l, n, 0, 0)),
        out_shape=jax.ShapeDtypeStruct((depth, 6, rows, d), F32),
        compiler_params=_cparams(("parallel", "parallel")),
        name="adaln_mod",
    )(s, w_ada, b3)


def _sincos_1d(pos, dim):
    half = dim // 2
    omega = 1.0 / (10000.0 ** (jnp.arange(half, dtype=F32) / half))
    ang = pos[:, None] * omega[None, :]
    return jnp.concatenate([jnp.sin(ang), jnp.cos(ang)], axis=-1)


def _grid_pos_embed(rows, dim):
    emb_r = _sincos_1d(jnp.arange(rows, dtype=F32), dim // 2)
    emb_c = _sincos_1d(jnp.arange(GRID_W, dtype=F32), dim // 2)
    pe = jnp.concatenate([
        jnp.broadcast_to(emb_r[:, None, :], (rows, GRID_W, dim // 2)),
        jnp.broadcast_to(emb_c[None, :, :], (rows, GRID_W, dim // 2))], axis=-1)
    return pe.reshape(rows * GRID_W, dim)


def _embed_kernel(x_ref, pe_ref, o_ref):
    o_ref[0] = x_ref[0] + pe_ref[...]


def _embed(x, tt):
    n_b, n_l, d = x.shape
    pe = _grid_pos_embed(n_l // GRID_W, d)
    return pl.pallas_call(
        _embed_kernel,
        grid=(n_b, n_l // tt),
        in_specs=[pl.BlockSpec((1, tt, d), lambda b, j: (b, j, 0)),
                  pl.BlockSpec((tt, d), lambda b, j: (j, 0))],
        out_specs=pl.BlockSpec((1, tt, d), lambda b, j: (b, j, 0)),
        out_shape=jax.ShapeDtypeStruct(x.shape, x.dtype),
        compiler_params=_cparams(("parallel", "parallel")),
        name="pos_embed",
    )(x, pe)


def _zs_kernel(x_ref, mod_ref, g_ref, w_ref, b_ref, o_ref):
    h = _prenorm(x_ref[0], g_ref[0:1], mod_ref[1, 0], mod_ref[0, 0])
    o_ref[0] = _dot(h.astype(BF16), w_ref[...]) + b_ref[...]


def _mod_spec(d, row_of_batch):
    return pl.BlockSpec((6, 1, 1, d), lambda b, j: (0, row_of_batch(b), 0, 0))


def _zs_pass(x, mod4, row_of_batch, norm_g, w_zs, b_zs, tt):
    n_b, n_l, d = x.shape
    return pl.pallas_call(
        _zs_kernel,
        grid=(n_b, n_l // tt),
        in_specs=[pl.BlockSpec((1, tt, d), lambda b, j: (b, j, 0)),
                  _mod_spec(d, row_of_batch),
                  _const_spec(norm_g.shape), _const_spec(w_zs.shape), _const_spec(b_zs.shape)],
        out_specs=pl.BlockSpec((1, tt, W_BR), lambda b, j: (b, j, 0)),
        out_shape=jax.ShapeDtypeStruct((n_b, n_l, W_BR), F32),
        compiler_params=_cparams(("parallel", "parallel")),
        name="ssm_in_proj",
    )(x, mod4, norm_g, w_zs, b_zs)


def _ssm_chunk_matrices(lam_re, lam_im, log_dt, b_re, b_im, c_re, c_im):
    hp = lax.Precision.HIGHEST
    n_g = lam_re.shape[1]
    n_slab = n_g // SLAB_GROUPS
    q = SSM_Q
    dt = jnp.exp(log_dt)[..., None]
    a_mag = jnp.exp(lam_re * dt)
    ang = lam_im * dt
    a_re, a_im = a_mag * jnp.cos(ang), a_mag * jnp.sin(ang)
    den = lam_re * lam_re + lam_im * lam_im
    f_re = ((a_re - 1.0) * lam_re + a_im * lam_im) / den
    f_im = (a_im * lam_re - (a_re - 1.0) * lam_im) / den
    bb_re = f_re[..., None] * b_re - f_im[..., None] * b_im
    bb_im = f_re[..., None] * b_im + f_im[..., None] * b_re
    tau = jnp.arange(q + 1, dtype=F32)[:, None, None, None]
    pw_mag = jnp.exp(tau * (lam_re * dt)[None])
    pw_re = pw_mag * jnp.cos(tau * ang[None])
    pw_im = pw_mag * jnp.sin(tau * ang[None])
    cp_re = c_re[None] * pw_re[:, :, :, None, :] - c_im[None] * pw_im[:, :, :, None, :]
    cp_im = c_re[None] * pw_im[:, :, :, None, :] + c_im[None] * pw_re[:, :, :, None, :]
    kk = (jnp.einsum('tdgjp,dgpi->tdgji', cp_re, bb_re, precision=hp)
          - jnp.einsum('tdgjp,dgpi->tdgji', cp_im, bb_im, precision=hp))
    pb_re = pw_re[..., None] * bb_re[None] - pw_im[..., None] * bb_im[None]
    pb_im = pw_re[..., None] * bb_im[None] + pw_im[..., None] * bb_re[None]

    eye = jnp.eye(SLAB_GROUPS, dtype=F32)
    r = jnp.arange(q)
    ms, ps, ns = [], [], []
    for d in range(2):
        lag = (r[None, :] - r[:, None]) if d == 0 else (r[:, None] - r[None, :])
        valid = (lag >= 0).astype(F32)
        t = kk[jnp.clip(lag, 0, q - 1), d] * valid[:, :, None, None, None]
        t = t.reshape(q, q, n_slab, SLAB_GROUPS, SSM_GROUP, SSM_GROUP)
        m = jnp.einsum('rskgji,gh->krgishj', t, eye).reshape(n_slab, FLAT, FLAT)
        p_pow = (q - 1 - r) if d == 0 else r
        pr = pb_re[p_pow, d].reshape(q, n_slab, SLAB_GROUPS, SSM_STATE, SSM_GROUP)
        pi = pb_im[p_pow, d].reshape(q, n_slab, SLAB_GROUPS, SSM_STATE, SSM_GROUP)
        p = jnp.stack([jnp.einsum('rkgpi,gh->krgihp', pr, eye),
                       jnp.einsum('rkgpi,gh->krgihp', pi, eye)], axis=4)
        p = p.reshape(n_slab, FLAT, 2 * SLAB_STATE)
        n_pow = (r + 1) if d == 0 else (q - r)
        nr = cp_re[n_pow, d].reshape(q, n_slab, SLAB_GROUPS, SSM_GROUP, SSM_STATE)
        ni = cp_im[n_pow, d].reshape(q, n_slab, SLAB_GROUPS, SSM_GROUP, SSM_STATE)
        n = jnp.stack([jnp.einsum('skgjp,gh->kgpshj', nr, eye),
                       -jnp.einsum('skgjp,gh->kgpshj', ni, eye)], axis=1)
        n = n.reshape(n_slab, 2 * SLAB_STATE, FLAT)
        ms.append(m)
        ps.append(p)
        ns.append(n)
    aq = jnp.concatenate([pw_re[q].reshape(2, n_slab, 1, SLAB_STATE),
                          pw_im[q].reshape(2, n_slab, 1, SLAB_STATE)], axis=-1)
    return (jnp.stack(ms).astype(BF16), jnp.stack(ps).astype(BF16), jnp.stack(ns).astype(BF16), aq)


def _ssm_kernel(zc_ref, zx_ref, m_ref, p_ref, n_ref, a_ref, yc_ref, yx_ref, u_scr, s_scr, *, ncc, ncx, nb):
    d = pl.program_id(2)
    nc = ncc + ncx
    q = SSM_Q

    @pl.when(d == 0)
    def _():
        for bb in range(nb):
            for k in range(q):
                lanes = slice(k * LANES, (k + 1) * LANES)
                u_scr[bb * nc:bb * nc + ncc, lanes] = zc_ref[bb, pl.ds(k, ncc, stride=q), :].astype(BF16)
                u_scr[bb * nc + ncc:(bb + 1) * nc, lanes] = zx_ref[bb, pl.ds(k, ncx, stride=q), :].astype(BF16)

    u = u_scr[...]
    s = _dot(u, p_ref[0, 0])
    nlb = FLAT // LANES
    for kb in range(nlb):
        s_scr[kb] = s[:, kb * LANES:(kb + 1) * LANES]

    a = a_ref[0, 0]
    ar, ai = a[:, :SLAB_STATE], a[:, SLAB_STATE:]

    def step(t, h):
        row = jnp.where(d == 0, t, jnp.where(t < ncc, ncc - 1 - t, nc + ncc - 1 - t))
        idx = pl.ds(row, nb, stride=nc)
        s_t = jnp.concatenate([s_scr[kb, idx, :] for kb in range(nlb)], axis=-1)
        for kb in range(nlb):
            s_scr[kb, idx, :] = h[:, kb * LANES:(kb + 1) * LANES]
        hr, hi = h[:, :SLAB_STATE], h[:, SLAB_STATE:]
        nr = ar * hr - ai * hi + s_t[:, :SLAB_STATE]
        ni = ar * hi + ai * hr + s_t[:, SLAB_STATE:]
        return jnp.concatenate([nr, ni], axis=-1)

    lax.fori_loop(0, nc, step, jnp.zeros((nb, 2 * SLAB_STATE), F32))

    h_in = jnp.concatenate([s_scr[kb] for kb in range(nlb)], axis=-1).astype(BF16)
    y = _dot(u, m_ref[0, 0]) + _dot(h_in, n_ref[0, 0])

    def emit(first):
        for bb in range(nb):
            for k in range(q):
                lanes = slice(k * LANES, (k + 1) * LANES)
                ic = (bb, pl.ds(k, ncc, stride=q), slice(None))
                ix = (bb, pl.ds(k, ncx, stride=q), slice(None))
                pc = y[bb * nc:bb * nc + ncc, lanes]
                px = y[bb * nc + ncc:(bb + 1) * nc, lanes]
                if first:
                    yc_ref[ic] = pc
                    yx_ref[ix] = px
                else:
                    yc_ref[ic] = yc_ref[ic] + pc
                    yx_ref[ix] = yx_ref[ix] + px

    @pl.when(d == 0)
    def _():
        emit(True)

    @pl.when(d == 1)
    def _():
        emit(False)


def _ssm_scan(zs_c, zs_x, mats, nb):
    m, p, n, aq = mats
    n_b, ctx_len, w = zs_c.shape
    n_l = zs_x.shape[1]
    n_slab = w // LANES
    ncc, ncx = ctx_len // SSM_Q, n_l // SSM_Q
    rows = nb * (ncc + ncx)
    mat_spec = pl.BlockSpec((1, 1, FLAT, FLAT), lambda k, g, d: (d, k, 0, 0))
    return pl.pallas_call(
        functools.partial(_ssm_kernel, ncc=ncc, ncx=ncx, nb=nb),
        grid=(n_slab, n_b // nb, 2),
        in_specs=[pl.BlockSpec((nb, ctx_len, LANES), lambda k, g, d: (g, 0, k)),
                  pl.BlockSpec((nb, n_l, LANES), lambda k, g, d: (g, 0, k)),
                  mat_spec, mat_spec, mat_spec,
                  pl.BlockSpec((1, 1, 1, FLAT), lambda k, g, d: (d, k, 0, 0))],
        out_specs=[pl.BlockSpec((nb, ctx_len, LANES), lambda k, g, d: (g, 0, k)),
                   pl.BlockSpec((nb, n_l, LANES), lambda k, g, d: (g, 0, k))],
        out_shape=[jax.ShapeDtypeStruct(zs_c.shape, F32), jax.ShapeDtypeStruct(zs_x.shape, F32)],
        scratch_shapes=[pltpu.VMEM((rows, FLAT), BF16), pltpu.VMEM((FLAT // LANES, rows, LANES), F32)],
        compiler_params=_cparams(("arbitrary", "arbitrary", "arbitrary")),
        name="s5_scan",
    )(zs_c, zs_x, m, p, n, aq)


def _mixer_kernel(xc_ref, xp_ref, xn_ref, ys_ref, mod_ref, g_ref, w_in_ref, b_in_ref, cw_ref, v512_ref,
                  cwo_ref, sw_ref, sbias_ref, swo_ref, wglu_ref, wso_ref, wo_ref, v1024_ref,
                  o_ref, hext, aext, *, tt, nt):
    j = pl.program_id(1)
    w = W_BR
    d = xc_ref.shape[-1]
    x = xc_ref[0]
    g0 = g_ref[0:1]
    shift, scale, gate = mod_ref[0, 0], mod_ref[1, 0], mod_ref[2, 0]

    def pre(v):
        return _prenorm(v, g0, scale, shift).astype(BF16)

    hext[0:HALO] = pre(xp_ref[0])
    hext[HALO:HALO + tt] = pre(x)
    hext[HALO + tt:] = pre(xn_ref[0])

    za = _dot(hext[...], w_in_ref[:, 0:2 * w]) + b_in_ref[:, 0:2 * w]
    a = za[:, :w] * jax.nn.sigmoid(za[:, w:])
    row = lax.broadcasted_iota(jnp.int32, (tt + 2 * HALO, 1), 0)
    has_prev = (j > 0).astype(F32)
    has_next = (j < nt - 1).astype(F32)
    aext[...] = a * jnp.where(row < HALO, has_prev, jnp.where(row >= HALO + tt, has_next, 1.0))
    acc = jnp.broadcast_to(v512_ref[0:1], (tt, w))
    for k in range(CONV_K):
        acc = acc + cw_ref[k:k + 1, :] * aext[pl.ds(k + HALO - CONV_K // 2, tt), :]
    a2 = _layernorm(acc, v512_ref[1:2], v512_ref[2:3])
    a2 = a2 * jax.nn.sigmoid(a2)
    o_a = _dot(a2.astype(BF16), cwo_ref[...]) + v1024_ref[0:1]

    h = hext[HALO:HALO + tt]

    zb = jax.nn.gelu(_dot(h, w_in_ref[:, 2 * w:4 * w]) + b_in_ref[:, 2 * w:4 * w])
    u_b = zb[:, :w]
    v_b = _layernorm(zb[:, w:], v512_ref[3:4], v512_ref[4:5]).astype(BF16)
    lane = lax.broadcasted_iota(jnp.int32, (CHUNK, LANES), 1)
    lo = lane < (w // SGU_HEADS)
    zero = jnp.zeros((CHUNK, LANES), BF16)
    row_blocks = []
    for c in range(tt // CHUNK):
        cols = []
        for hp in range(w // LANES):
            vp = v_b[c * CHUNK:(c + 1) * CHUNK, hp * LANES:(hp + 1) * LANES]
            cols.append(_dot(sw_ref[2 * hp], jnp.where(lo, vp, zero))
                        + _dot(sw_ref[2 * hp + 1], jnp.where(lo, zero, vp))
                        + sbias_ref[:, hp * LANES:(hp + 1) * LANES])
        row_blocks.append(jnp.concatenate(cols, axis=1))
    sp = jnp.concatenate(row_blocks, axis=0)
    o_b = _dot((u_b * sp).astype(BF16), swo_ref[...]) + v1024_ref[1:2]

    zs = _dot(h, w_in_ref[:, 4 * w:5 * w]) + b_in_ref[:, 4 * w:5 * w]
    y = jax.nn.gelu(ys_ref[0] + v512_ref[5:6] * zs)
    y = y * jax.nn.sigmoid(_dot(y.astype(BF16), wglu_ref[...]) + v512_ref[6:7])
    o_c = _dot(y.astype(BF16), wso_ref[...]) + v1024_ref[2:3]

    zg = jax.nn.sigmoid(_dot(h, w_in_ref[:, 5 * w:]) + b_in_ref[:, 5 * w:])
    merged = zg[:, :d] * o_a + zg[:, d:2 * d] * o_b + zg[:, 2 * d:] * o_c
    out = _dot(merged.astype(BF16), wo_ref[...]) + v1024_ref[3:4]
    o_ref[0] = x + gate * (_rms(out) * g_ref[1:2])


def _mixer_pass(x, ysum, mod4, row_of_batch, wts, tt):
    n_b, n_l, d = x.shape
    nt = n_l // tt
    hb = tt // HALO
    last_halo = n_l // HALO - 1
    in_specs = [
        pl.BlockSpec((1, tt, d), lambda b, j: (b, j, 0)),
        pl.BlockSpec((1, HALO, d), lambda b, j: (b, jnp.maximum(j * hb - 1, 0), 0)),
        pl.BlockSpec((1, HALO, d), lambda b, j: (b, jnp.minimum((j + 1) * hb, last_halo), 0)),
        pl.BlockSpec((1, tt, W_BR), lambda b, j: (b, j, 0)),
        _mod_spec(d, row_of_batch),
    ] + [_const_spec(w.shape) for w in wts]
    return pl.pallas_call(
        functools.partial(_mixer_kernel, tt=tt, nt=nt),
        grid=(n_b, nt),
        in_specs=in_specs,
        out_specs=pl.BlockSpec((1, tt, d), lambda b, j: (b, j, 0)),
        out_shape=jax.ShapeDtypeStruct(x.shape, x.dtype),
        scratch_shapes=[pltpu.VMEM((tt + 2 * HALO, d), BF16), pltpu.VMEM((tt + 2 * HALO, W_BR), F32)],
        compiler_params=_cparams(("parallel", "parallel")),
        name="mixer",
    )(x, x, x, ysum, mod4, *wts)


def _ffn_kernel(x_ref, mod_ref, g_ref, wup_ref, wdn_ref, o_ref, *, blk):
    x = x_ref[0]
    d_ff = wdn_ref.shape[0]
    h = _prenorm(x, g_ref[2:3], mod_ref[4, 0], mod_ref[3, 0]).astype(BF16)
    acc = jnp.zeros(x.shape, F32)
    for n in range(d_ff // blk):
        gg = _dot(h, wup_ref[:, n * blk:(n + 1) * blk])
        uu = _dot(h, wup_ref[:, d_ff + n * blk:d_ff + (n + 1) * blk])
        act = (gg * jax.nn.sigmoid(gg)) * uu
        acc = acc + _dot(act.astype(BF16), wdn_ref[n * blk:(n + 1) * blk, :])
    o_ref[0] = x + mod_ref[5, 0] * (_rms(acc) * g_ref[3:4])


def _ffn_pass(x, mod4, row_of_batch, norm_g, w_up, w_down, tt):
    n_b, n_l, d = x.shape
    return pl.pallas_call(
        functools.partial(_ffn_kernel, blk=256),
        grid=(n_b, n_l // tt),
        in_specs=[pl.BlockSpec((1, tt, d), lambda b, j: (b, j, 0)),
                  _mod_spec(d, row_of_batch),
                  _const_spec(norm_g.shape), _const_spec(w_up.shape), _const_spec(w_down.shape)],
        out_specs=pl.BlockSpec((1, tt, d), lambda b, j: (b, j, 0)),
        out_shape=jax.ShapeDtypeStruct(x.shape, x.dtype),
        compiler_params=_cparams(("parallel", "parallel")),
        name="ffn",
    )(x, mod4, norm_g, w_up, w_down)


def kernel(x, c, ctx, c_ctx, w_ada, b_ada, norm_g, w_in, b_in, conv_w, conv_b, conv_ln_g, conv_ln_b,
           conv_w_out, conv_b_out, sgu_ln_g, sgu_ln_b, sgu_w, sgu_b, sgu_w_out, sgu_b_out,
           ssm_lam_re, ssm_lam_im, ssm_log_dt, ssm_b_re, ssm_b_im, ssm_c_re, ssm_c_im, ssm_d,
           ssm_w_glu, ssm_b_glu, ssm_w_out, ssm_b_out, w_o, b_o, ffn_w_up, ffn_w_down):
    n_b, n_l, d = x.shape
    ctx_len = ctx.shape[1]
    depth = w_ada.shape[0]
    w = W_BR
    tt_x = 256
    tt_c = 256
    assert n_l % tt_x == 0 and ctx_len % tt_c == 0 and n_l % GRID_W == 0
    assert tt_x % CHUNK == 0 and tt_c % CHUNK == 0 and n_b % 2 == 0

    mod = _modulation(c, c_ctx, w_ada, b_ada)
    x = _embed(x, tt_x)
    row_x = lambda b: b
    row_c = lambda b: n_b

    for l in range(depth):
        last = l == depth - 1
        mod4 = mod[l].reshape(6, mod.shape[2], 1, d)
        w_in_l = w_in[l].astype(BF16)
        b_in_l = b_in[l].reshape(1, -1)
        w_zs = w_in_l[:, 4 * w:5 * w]
        b_zs = b_in_l[:, 4 * w:5 * w]
        zeros512 = jnp.zeros((w,), F32)
        v512 = jnp.stack([conv_b[l], conv_ln_g[l], conv_ln_b[l], sgu_ln_g[l], sgu_ln_b[l], ssm_d[l],
                          ssm_b_glu[l], zeros512])
        zeros1024 = jnp.zeros((d,), F32)
        v1024 = jnp.stack([conv_b_out[l], sgu_b_out[l], ssm_b_out[l], b_o[l]] + [zeros1024] * 4)
        sbias = jnp.repeat(sgu_b[l].T, w // SGU_HEADS, axis=1)
        wts = [norm_g[l], w_in_l, b_in_l, conv_w[l], v512, conv_w_out[l].astype(BF16),
               sgu_w[l].astype(BF16), sbias, sgu_w_out[l].astype(BF16), ssm_w_glu[l].astype(BF16),
               ssm_w_out[l].astype(BF16), w_o[l].astype(BF16), v1024]
        mats = _ssm_chunk_matrices(ssm_lam_re[l], ssm_lam_im[l], ssm_log_dt[l], ssm_b_re[l], ssm_b_im[l],
                                   ssm_c_re[l], ssm_c_im[l])
        w_up_l = ffn_w_up[l].astype(BF16)
        w_dn_l = ffn_w_down[l].astype(BF16)

        zs_c = _zs_pass(ctx, mod4, row_c, norm_g[l], w_zs, b_zs, tt_c)
        zs_x = _zs_pass(x, mod4, row_x, norm_g[l], w_zs, b_zs, tt_x)
        y_c, y_x = _ssm_scan(zs_c, zs_x, mats, nb=2)
        if not last:
            ctx = _mixer_pass(ctx, y_c, mod4, row_c, wts, tt_c)
            ctx = _ffn_pass(ctx, mod4, row_c, norm_g[l], w_up_l, w_dn_l, tt_c)
        x = _mixer_pass(x, y_x, mod4, row_x, wts, tt_x)
        x = _ffn_pass(x, mod4, row_x, norm_g[l], w_up_l, w_dn_l, tt_x)
    return x
```

```python
import functools

import jax
import jax.numpy as jnp
from jax import lax
from jax.experimental import pallas as pl
from jax.experimental.pallas import tpu as pltpu

EPS = 1e-6
GRID_W = 64
W_BR = 512
CONV_K = 31
CHUNK = 128
SGU_HEADS = 8
SSM_GROUP = 16
SSM_STATE = 64

LANES = 128
HALO = 16
SSM_Q = 8
SLAB_GROUPS = LANES // SSM_GROUP
SLAB_STATE = SLAB_GROUPS * SSM_STATE
FLAT = SSM_Q * LANES
VMEM_LIMIT = 56 * 1024 * 1024

BF16 = jnp.bfloat16
F32 = jnp.float32


def _cparams(sem):
    return pltpu.CompilerParams(dimension_semantics=sem, vmem_limit_bytes=VMEM_LIMIT)


def _const_spec(shape):
    nd = len(shape)
    return pl.BlockSpec(shape, lambda *_: (0,) * nd)


def _rms(x):
    return x * lax.rsqrt(jnp.mean(x * x, axis=-1, keepdims=True) + EPS)


def _prenorm(x, g, scale, shift):
    return (_rms(x) * g) * (1.0 + scale) + shift


def _layernorm(x, g, b):
    mu = jnp.mean(x, axis=-1, keepdims=True)
    xc = x - mu
    var = jnp.mean(xc * xc, axis=-1, keepdims=True)
    return (xc * lax.rsqrt(var + EPS)) * g + b


def _dot(a, b):
    return jnp.dot(a, b, preferred_element_type=F32)


def _dot_nt(a, b, precision=None):
    return lax.dot_general(a, b, (((1,), (1,)), ((), ())), precision=precision,
                           preferred_element_type=F32)


def _mod_kernel(s_ref, w_ref, b_ref, o_ref):
    s = s_ref[...]
    s = s * jax.nn.sigmoid(s)
    o_ref[0, 0] = _dot(s, w_ref[0]) + b_ref[0, 0]


def _modulation(c, c_ctx, w_ada, b_ada):
    depth, d, _ = w_ada.shape
    n_b = c.shape[0]
    rows = -(-(n_b + 1) // 8) * 8
    s = jnp.zeros((rows, d), F32).at[:n_b].set(c).at[n_b].set(c_ctx)
    b3 = b_ada.reshape(depth, 6, 1, d)
    return pl.pallas_call(
        _mod_kernel,
        grid=(depth, 6),
        in_specs=[
            pl.BlockSpec((rows, d), lambda l, n: (0, 0)),
            pl.BlockSpec((1, d, d), lambda l, n: (l, 0, n)),
            pl.BlockSpec((1, 1, 1, d), lambda l, n: (l, n, 0, 0)),
        ],
        out_specs=pl.BlockSpec((1, 1, rows, d), lambda l, n: (l, n, 0, 0)),
        out_shape=jax.ShapeDtypeStruct((depth, 6, rows, d), F32),
        compiler_params=_cparams(("parallel", "parallel")),
        name="adaln_mod",
    )(s, w_ada, b3)


def _sincos_1d(pos, dim):
    half = dim // 2
    omega = 1.0 / (10000.0 ** (jnp.arange(half, dtype=F32) / half))
    ang = pos[:, None] * omega[None, :]
    return jnp.concatenate([jnp.sin(ang), jnp.cos(ang)], axis=-1)


def _grid_pos_embed(rows, dim):
    emb_r = _sincos_1d(jnp.arange(rows, dtype=F32), dim // 2)
    emb_c = _sincos_1d(jnp.arange(GRID_W, dtype=F32), dim // 2)
    pe = jnp.concatenate([
        jnp.broadcast_to(emb_r[:, None, :], (rows, GRID_W, dim // 2)),
        jnp.broadcast_to(emb_c[None, :, :], (rows, GRID_W, dim // 2))], axis=-1)
    return pe.reshape(rows * GRID_W, dim)


def _embed_kernel(x_ref, pe_ref, o_ref):
    o_ref[0] = x_ref[0] + pe_ref[...]


def _embed(x, tt):
    n_b, n_l, d = x.shape
    pe = _grid_pos_embed(n_l // GRID_W, d)
    return pl.pallas_call(
        _embed_kernel,
        grid=(n_b, n_l // tt),
        in_specs=[pl.BlockSpec((1, tt, d), lambda b, j: (b, j, 0)),
                  pl.BlockSpec((tt, d), lambda b, j: (j, 0))],
        out_specs=pl.BlockSpec((1, tt, d), lambda b, j: (b, j, 0)),
        out_shape=jax.ShapeDtypeStruct(x.shape, x.dtype),
        compiler_params=_cparams(("parallel", "parallel")),
        name="pos_embed",
    )(x, pe)


def _zs_kernel(x_ref, mod_ref, g_ref, w_ref, b_ref, o_ref):
    h = _prenorm(x_ref[0], g_ref[0:1], mod_ref[1, 0], mod_ref[0, 0])
    o_ref[0] = _dot(h.astype(BF16), w_ref[...]) + b_ref[...]


def _mod_spec(d, row_of_batch):
    return pl.BlockSpec((6, 1, 1, d), lambda b, j: (0, row_of_batch(b), 0, 0))


def _zs_pass(x, mod4, row_of_batch, norm_g, w_zs, b_zs, tt):
    n_b, n_l, d = x.shape
    return pl.pallas_call(
        _zs_kernel,
        grid=(n_b, n_l // tt),
        in_specs=[pl.BlockSpec((1, tt, d), lambda b, j: (b, j, 0)),
                  _mod_spec(d, row_of_batch),
                  _const_spec(norm_g.shape), _const_spec(w_zs.shape), _const_spec(b_zs.shape)],
        out_specs=pl.BlockSpec((1, tt, W_BR), lambda b, j: (b, j, 0)),
        out_shape=jax.ShapeDtypeStruct((n_b, n_l, W_BR), F32),
        compiler_params=_cparams(("parallel", "parallel")),
        name="ssm_in_proj",
    )(x, mod4, norm_g, w_zs, b_zs)


def _ssm_compact_params(lam_re, lam_im, log_dt, b_re, b_im, c_re, c_im):
    depth, _, n_g, n_p = lam_re.shape
    n_slab = n_g // SLAB_GROUPS

    def vec(v):
        return v.reshape(depth, 2, n_slab, SLAB_STATE).transpose(0, 2, 1, 3)

    lam = jnp.stack([vec(lam_re), vec(lam_im),
                     vec(jnp.broadcast_to(log_dt[..., None], lam_re.shape))], axis=3)

    def mat(re, im, chan_axis_last):
        t = jnp.stack([re, im], axis=2)
        t = t.reshape(depth, 2, 2, n_slab, SLAB_GROUPS, *t.shape[4:])
        if chan_axis_last:
            t = t.transpose(0, 3, 1, 2, 6, 4, 5)
        else:
            t = t.transpose(0, 3, 1, 2, 5, 4, 6)
        return t.reshape(depth, n_slab, 2, 2, SSM_GROUP, SLAB_STATE)

    return lam, mat(b_re, b_im, True), mat(c_re, c_im, False)


def _build_chunk_matrices(lam_ref, bt_ref, ct_ref, msum_scr, wp_scr, nt_scr, aq_scr):
    q = SSM_Q
    hp = lax.Precision.HIGHEST
    rowg = lax.broadcasted_iota(jnp.int32, (LANES, SLAB_STATE), 0) // SSM_GROUP
    colg = lax.broadcasted_iota(jnp.int32, (LANES, SLAB_STATE), 1) // SSM_STATE
    same_group = rowg == colg

    def block_diag(v):
        return jnp.where(same_group, jnp.concatenate([v] * SLAB_GROUPS, axis=0), 0.0)

    kk = []
    for d in range(2):
        lam_re, lam_im = lam_ref[0, d, 0:1, :], lam_ref[0, d, 1:2, :]
        dt = jnp.exp(lam_ref[0, d, 2:3, :])
        ang = lam_im * dt
        a_mag = jnp.exp(lam_re * dt)
        a_re, a_im = a_mag * jnp.cos(ang), a_mag * jnp.sin(ang)
        den = lam_re * lam_re + lam_im * lam_im
        f_re = ((a_re - 1.0) * lam_re + a_im * lam_im) / den
        f_im = (a_im * lam_re - (a_re - 1.0) * lam_im) / den
        bt_re, bt_im = bt_ref[0, d, 0], bt_ref[0, d, 1]
        bb_re = f_re * bt_re - f_im * bt_im
        bb_im = f_re * bt_im + f_im * bt_re
        ct_re, ct_im = ct_ref[0, d, 0], ct_ref[0, d, 1]
        pw = []
        for tau in range(q + 1):
            mag = jnp.exp(float(tau) * (lam_re * dt))
            pw.append((mag * jnp.cos(float(tau) * ang), mag * jnp.sin(float(tau) * ang)))
        cps = []
        for tau in range(q + 1):
            pr, pi = pw[tau]
            cps.append((block_diag(ct_re * pr - ct_im * pi), block_diag(ct_re * pi + ct_im * pr)))
        bbd_re, bbd_im = block_diag(bb_re), block_diag(bb_im)
        kk.append([_dot_nt(bbd_re, cps[tau][0], hp) - _dot_nt(bbd_im, cps[tau][1], hp) for tau in range(q)])
        lo = d * 2 * SLAB_STATE
        for r in range(q):
            pr, pi = pw[q - 1 - r] if d == 0 else pw[r]
            rows = slice(r * LANES, (r + 1) * LANES)
            wp_scr[rows, lo:lo + SLAB_STATE] = block_diag(pr * bb_re - pi * bb_im).astype(BF16)
            wp_scr[rows, lo + SLAB_STATE:lo + 2 * SLAB_STATE] = block_diag(pr * bb_im + pi * bb_re).astype(BF16)
            cr, ci = cps[r + 1] if d == 0 else cps[q - r]
            nt_scr[rows, lo:lo + SLAB_STATE] = cr.astype(BF16)
            nt_scr[rows, lo + SLAB_STATE:lo + 2 * SLAB_STATE] = (-ci).astype(BF16)
        aq_scr[d:d + 1, 0:SLAB_STATE] = pw[q][0]
        aq_scr[d:d + 1, SLAB_STATE:] = pw[q][1]
    for r in range(q):
        for s in range(q):
            if s > r:
                blk = kk[0][s - r]
            elif s < r:
                blk = kk[1][r - s]
            else:
                blk = kk[0][0] + kk[1][0]
            msum_scr[r * LANES:(r + 1) * LANES, s * LANES:(s + 1) * LANES] = blk.astype(BF16)


def _ssm_kernel(zc_ref, zx_ref, lam_ref, bt_ref, ct_ref, yc_ref, yx_ref,
                u_scr, s_scr, msum_scr, wp_scr, nt_scr, aq_scr, *, ncc, ncx, nb):
    nc = ncc + ncx
    q = SSM_Q
    nlb = 2 * SLAB_STATE // LANES

    @pl.when(pl.program_id(1) == 0)
    def _():
        _build_chunk_matrices(lam_ref, bt_ref, ct_ref, msum_scr, wp_scr, nt_scr, aq_scr)

    for bb in range(nb):
        for k in range(q):
            lanes = slice(k * LANES, (k + 1) * LANES)
            u_scr[bb * nc:bb * nc + ncc, lanes] = zc_ref[bb, pl.ds(k, ncc, stride=q), :].astype(BF16)
            u_scr[bb * nc + ncc:(bb + 1) * nc, lanes] = zx_ref[bb, pl.ds(k, ncx, stride=q), :].astype(BF16)

    for bb in range(nb):
        rows = slice(bb * nc, (bb + 1) * nc)
        s = _dot(u_scr[rows, :], wp_scr[...])
        for lb in range(2 * nlb):
            s_scr[lb, rows, :] = s[:, lb * LANES:(lb + 1) * LANES]

    af, ab = aq_scr[0:1, :], aq_scr[1:2, :]

    def advance(a, h, s_t):
        ar, ai = a[:, :SLAB_STATE], a[:, SLAB_STATE:]
        hr, hi = h[:, :SLAB_STATE], h[:, SLAB_STATE:]
        return jnp.concatenate([ar * hr - ai * hi + s_t[:, :SLAB_STATE],
                                ar * hi + ai * hr + s_t[:, SLAB_STATE:]], axis=-1)

    def step(t, carry):
        hf, hb = carry
        row_b = jnp.where(t < ncc, ncc - 1 - t, nc + ncc - 1 - t)
        idx_f = pl.ds(t, nb, stride=nc)
        idx_b = pl.ds(row_b, nb, stride=nc)
        sf = jnp.concatenate([s_scr[lb, idx_f, :] for lb in range(nlb)], axis=-1)
        sb = jnp.concatenate([s_scr[nlb + lb, idx_b, :] for lb in range(nlb)], axis=-1)
        for lb in range(nlb):
            s_scr[lb, idx_f, :] = hf[:, lb * LANES:(lb + 1) * LANES]
            s_scr[nlb + lb, idx_b, :] = hb[:, lb * LANES:(lb + 1) * LANES]
        return advance(af, hf, sf), advance(ab, hb, sb)

    h0 = jnp.zeros((nb, 2 * SLAB_STATE), F32)
    lax.fori_loop(0, nc, step, (h0, h0))

    for bb in range(nb):
        rows = slice(bb * nc, (bb + 1) * nc)
        h_in = jnp.concatenate([s_scr[lb, rows, :] for lb in range(2 * nlb)], axis=-1).astype(BF16)
        y = _dot(u_scr[rows, :], msum_scr[...]) + _dot_nt(h_in, nt_scr[...])
        for k in range(q):
            lanes = slice(k * LANES, (k + 1) * LANES)
            yc_ref[bb, pl.ds(k, ncc, stride=q), :] = y[0:ncc, lanes]
            yx_ref[bb, pl.ds(k, ncx, stride=q), :] = y[ncc:nc, lanes]


def _ssm_scan(zs_c, zs_x, lam, b_t, c_t, nb):
    n_b, ctx_len, w = zs_c.shape
    n_l = zs_x.shape[1]
    n_slab = w // LANES
    ncc, ncx = ctx_len // SSM_Q, n_l // SSM_Q
    rows = nb * (ncc + ncx)
    par_spec = pl.BlockSpec((1, 2, 2, SSM_GROUP, SLAB_STATE), lambda k, g: (k, 0, 0, 0, 0))
    return pl.pallas_call(
        functools.partial(_ssm_kernel, ncc=ncc, ncx=ncx, nb=nb),
        grid=(n_slab, n_b // nb),
        in_specs=[pl.BlockSpec((nb, ctx_len, LANES), lambda k, g: (g, 0, k)),
                  pl.BlockSpec((nb, n_l, LANES), lambda k, g: (g, 0, k)),
                  pl.BlockSpec((1, 2, 3, SLAB_STATE), lambda k, g: (k, 0, 0, 0)),
                  par_spec, par_spec],
        out_specs=[pl.BlockSpec((nb, ctx_len, LANES), lambda k, g: (g, 0, k)),
                   pl.BlockSpec((nb, n_l, LANES), lambda k, g: (g, 0, k))],
        out_shape=[jax.ShapeDtypeStruct(zs_c.shape, F32), jax.ShapeDtypeStruct(zs_x.shape, F32)],
        scratch_shapes=[pltpu.VMEM((rows, FLAT), BF16),
                        pltpu.VMEM((4 * SLAB_STATE // LANES, rows, LANES), F32),
                        pltpu.VMEM((FLAT, FLAT), BF16),
                        pltpu.VMEM((FLAT, 4 * SLAB_STATE), BF16),
                        pltpu.VMEM((FLAT, 4 * SLAB_STATE), BF16),
                        pltpu.VMEM((8, 2 * SLAB_STATE), F32)],
        compiler_params=_cparams(("arbitrary", "arbitrary")),
        name="s5_scan",
    )(zs_c, zs_x, lam, b_t, c_t)


def _mixer_kernel(xc_ref, xp_ref, xn_ref, ys_ref, mod_ref, g_ref, w_in_ref, b_in_ref, cw_ref, v512_ref,
                  cwo_ref, sw_ref, sbias_ref, swo_ref, wglu_ref, wso_ref, wo_ref, v1024_ref,
                  o_ref, hext, aext, *, tt, nt):
    j = pl.program_id(1)
    w = W_BR
    d = xc_ref.shape[-1]
    x = xc_ref[0]
    g0 = g_ref[0:1]
    shift, scale, gate = mod_ref[0, 0], mod_ref[1, 0], mod_ref[2, 0]

    def pre(v):
        return _prenorm(v, g0, scale, shift).astype(BF16)

    hext[0:HALO] = pre(xp_ref[0])
    hext[HALO:HALO + tt] = pre(x)
    hext[HALO + tt:] = pre(xn_ref[0])

    za = _dot(hext[...], w_in_ref[:, 0:2 * w]) + b_in_ref[:, 0:2 * w]
    a = za[:, :w] * jax.nn.sigmoid(za[:, w:])
    row = lax.broadcasted_iota(jnp.int32, (tt + 2 * HALO, 1), 0)
    has_prev = (j > 0).astype(F32)
    has_next = (j < nt - 1).astype(F32)
    aext[...] = a * jnp.where(row < HALO, has_prev, jnp.where(row >= HALO + tt, has_next, 1.0))
    acc = jnp.broadcast_to(v512_ref[0:1], (tt, w))
    for k in range(CONV_K):
        acc = acc + cw_ref[k:k + 1, :] * aext[pl.ds(k + HALO - CONV_K // 2, tt), :]
    a2 = _layernorm(acc, v512_ref[1:2], v512_ref[2:3])
    a2 = a2 * jax.nn.sigmoid(a2)
    o_a = _dot(a2.astype(BF16), cwo_ref[...]) + v1024_ref[0:1]

    h = hext[HALO:HALO + tt]

    zb = jax.nn.gelu(_dot(h, w_in_ref[:, 2 * w:4 * w]) + b_in_ref[:, 2 * w:4 * w])
    u_b = zb[:, :w]
    v_b = _layernorm(zb[:, w:], v512_ref[3:4], v512_ref[4:5]).astype(BF16)
    lane = lax.broadcasted_iota(jnp.int32, (CHUNK, LANES), 1)
    lo = lane < (w // SGU_HEADS)
    zero = jnp.zeros((CHUNK, LANES), BF16)
    row_blocks = []
    for c in range(tt // CHUNK):
        cols = []
        for hp in range(w // LANES):
            vp = v_b[c * CHUNK:(c + 1) * CHUNK, hp * LANES:(hp + 1) * LANES]
            cols.append(_dot(sw_ref[2 * hp], jnp.where(lo, vp, zero))
                        + _dot(sw_ref[2 * hp + 1], jnp.where(lo, zero, vp))
                        + sbias_ref[:, hp * LANES:(hp + 1) * LANES])
        row_blocks.append(jnp.concatenate(cols, axis=1))
    sp = jnp.concatenate(row_blocks, axis=0)
    o_b = _dot((u_b * sp).astype(BF16), swo_ref[...]) + v1024_ref[1:2]

    zs = _dot(h, w_in_ref[:, 4 * w:5 * w]) + b_in_ref[:, 4 * w:5 * w]
    y = jax.nn.gelu(ys_ref[0] + v512_ref[5:6] * zs)
    y = y * jax.nn.sigmoid(_dot(y.astype(BF16), wglu_ref[...]) + v512_ref[6:7])
    o_c = _dot(y.astype(BF16), wso_ref[...]) + v1024_ref[2:3]

    zg = jax.nn.sigmoid(_dot(h, w_in_ref[:, 5 * w:]) + b_in_ref[:, 5 * w:])
    merged = zg[:, :d] * o_a + zg[:, d:2 * d] * o_b + zg[:, 2 * d:] * o_c
    out = _dot(merged.astype(BF16), wo_ref[...]) + v1024_ref[3:4]
    o_ref[0] = x + gate * (_rms(out) * g_ref[1:2])


def _mixer_pass(x, ysum, mod4, row_of_batch, wts, tt):
    n_b, n_l, d = x.shape
    nt = n_l // tt
    hb = tt // HALO
    last_halo = n_l // HALO - 1
    in_specs = [
        pl.BlockSpec((1, tt, d), lambda b, j: (b, j, 0)),
        pl.BlockSpec((1, HALO, d), lambda b, j: (b, jnp.maximum(j * hb - 1, 0), 0)),
        pl.BlockSpec((1, HALO, d), lambda b, j: (b, jnp.minimum((j + 1) * hb, last_halo), 0)),
        pl.BlockSpec((1, tt, W_BR), lambda b, j: (b, j, 0)),
        _mod_spec(d, row_of_batch),
    ] + [_const_spec(w.shape) for w in wts]
    return pl.pallas_call(
        functools.partial(_mixer_kernel, tt=tt, nt=nt),
        grid=(n_b, nt),
        in_specs=in_specs,
        out_specs=pl.BlockSpec((1, tt, d), lambda b, j: (b, j, 0)),
        out_shape=jax.ShapeDtypeStruct(x.shape, x.dtype),
        scratch_shapes=[pltpu.VMEM((tt + 2 * HALO, d), BF16), pltpu.VMEM((tt + 2 * HALO, W_BR), F32)],
        compiler_params=_cparams(("parallel", "parallel")),
        name="mixer",
    )(x, x, x, ysum, mod4, *wts)


def _ffn_kernel(x_ref, mod_ref, g_ref, wup_ref, wdn_ref, o_ref, *, blk):
    x = x_ref[0]
    d_ff = wdn_ref.shape[0]
    h = _prenorm(x, g_ref[2:3], mod_ref[4, 0], mod_ref[3, 0]).astype(BF16)
    acc = jnp.zeros(x.shape, F32)
    for n in range(d_ff // blk):
        gg = _dot(h, wup_ref[:, n * blk:(n + 1) * blk])
        uu = _dot(h, wup_ref[:, d_ff + n * blk:d_ff + (n + 1) * blk])
        act = (gg * jax.nn.sigmoid(gg)) * uu
        acc = acc + _dot(act.astype(BF16), wdn_ref[n * blk:(n + 1) * blk, :])
    o_ref[0] = x + mod_ref[5, 0] * (_rms(acc) * g_ref[3:4])


def _ffn_pass(x, mod4, row_of_batch, norm_g, w_up, w_down, tt):
    n_b, n_l, d = x.shape
    return pl.pallas_call(
        functools.partial(_ffn_kernel, blk=256),
        grid=(n_b, n_l // tt),
        in_specs=[pl.BlockSpec((1, tt, d), lambda b, j: (b, j, 0)),
                  _mod_spec(d, row_of_batch),
                  _const_spec(norm_g.shape), _const_spec(w_up.shape), _const_spec(w_down.shape)],
        out_specs=pl.BlockSpec((1, tt, d), lambda b, j: (b, j, 0)),
        out_shape=jax.ShapeDtypeStruct(x.shape, x.dtype),
        compiler_params=_cparams(("parallel", "parallel")),
        name="ffn",
    )(x, mod4, norm_g, w_up, w_down)


def kernel(x, c, ctx, c_ctx, w_ada, b_ada, norm_g, w_in, b_in, conv_w, conv_b, conv_ln_g, conv_ln_b,
           conv_w_out, conv_b_out, sgu_ln_g, sgu_ln_b, sgu_w, sgu_b, sgu_w_out, sgu_b_out,
           ssm_lam_re, ssm_lam_im, ssm_log_dt, ssm_b_re, ssm_b_im, ssm_c_re, ssm_c_im, ssm_d,
           ssm_w_glu, ssm_b_glu, ssm_w_out, ssm_b_out, w_o, b_o, ffn_w_up, ffn_w_down):
    n_b, n_l, d = x.shape
    ctx_len = ctx.shape[1]
    depth = w_ada.shape[0]
    w = W_BR
    tt_x = 256
    tt_c = 256
    assert n_l % tt_x == 0 and ctx_len % tt_c == 0 and n_l % GRID_W == 0
    assert tt_x % CHUNK == 0 and tt_c % CHUNK == 0 and n_b % 2 == 0

    mod = _modulation(c, c_ctx, w_ada, b_ada)
    x = _embed(x, tt_x)
    lam, b_t, c_t = _ssm_compact_params(ssm_lam_re, ssm_lam_im, ssm_log_dt, ssm_b_re, ssm_b_im,
                                        ssm_c_re, ssm_c_im)
    row_x = lambda b: b
    row_c = lambda b: n_b

    for l in range(depth):
        last = l == depth - 1
        mod4 = mod[l].reshape(6, mod.shape[2], 1, d)
        w_in_l = w_in[l].astype(BF16)
        b_in_l = b_in[l].reshape(1, -1)
        w_zs = w_in_l[:, 4 * w:5 * w]
        b_zs = b_in_l[:, 4 * w:5 * w]
        zeros512 = jnp.zeros((w,), F32)
        v512 = jnp.stack([conv_b[l], conv_ln_g[l], conv_ln_b[l], sgu_ln_g[l], sgu_ln_b[l], ssm_d[l],
                          ssm_b_glu[l], zeros512])
        zeros1024 = jnp.zeros((d,), F32)
        v1024 = jnp.stack([conv_b_out[l], sgu_b_out[l], ssm_b_out[l], b_o[l]] + [zeros1024] * 4)
        sbias = jnp.repeat(sgu_b[l].T, w // SGU_HEADS, axis=1)
        wts = [norm_g[l], w_in_l, b_in_l, conv_w[l], v512, conv_w_out[l].astype(BF16),
               sgu_w[l].astype(BF16), sbias, sgu_w_out[l].astype(BF16), ssm_w_glu[l].astype(BF16),
               ssm_w_out[l].astype(BF16), w_o[l].astype(BF16), v1024]
        w_up_l = ffn_w_up[l].astype(BF16)
        w_dn_l = ffn_w_down[l].astype(BF16)

        zs_c = _zs_pass(ctx, mod4, row_c, norm_g[l], w_zs, b_zs, tt_c)
        zs_x = _zs_pass(x, mod4, row_x, norm_g[l], w_zs, b_zs, tt_x)
        y_c, y_x = _ssm_scan(zs_c, zs_x, lam[l], b_t[l], c_t[l], nb=2)
        if not last:
            ctx = _mixer_pass(ctx, y_c, mod4, row_c, wts, tt_c)
            ctx = _ffn_pass(ctx, mod4, row_c, norm_g[l], w_up_l, w_dn_l, tt_c)
        x = _mixer_pass(x, y_x, mod4, row_x, wts, tt_x)
        x = _ffn_pass(x, mod4, row_x, norm_g[l], w_up_l, w_dn_l, tt_x)
    return x
```

```python
import functools

import jax
import jax.numpy as jnp
from jax import lax
from jax.experimental import pallas as pl
from jax.experimental.pallas import tpu as pltpu

EPS = 1e-6
GRID_W = 64
W_BR = 512
CONV_K = 31
CHUNK = 128
SGU_HEADS = 8
SSM_GROUP = 16
SSM_STATE = 64

LANES = 128
SUBLANES = 8
BF16_ROWS = 16
CONV_ROWS = 32
HALO = 16
SSM_Q = 8
SLAB_GROUPS = LANES // SSM_GROUP
SLAB_STATE = SLAB_GROUPS * SSM_STATE
FLAT = SSM_Q * LANES
VMEM_LIMIT = 56 * 1024 * 1024

BF16 = jnp.bfloat16
F32 = jnp.float32


def _cparams(sem):
    return pltpu.CompilerParams(dimension_semantics=sem, vmem_limit_bytes=VMEM_LIMIT)


def _const_spec(shape):
    nd = len(shape)
    return pl.BlockSpec(shape, lambda *_: (0,) * nd, pipeline_mode=pl.Buffered(1))


def _rms(x):
    return x * lax.rsqrt(jnp.mean(x * x, axis=-1, keepdims=True) + EPS)


def _prenorm(x, g, scale, shift):
    return (_rms(x) * g) * (1.0 + scale) + shift


def _layernorm(x, g, b):
    mu = jnp.mean(x, axis=-1, keepdims=True)
    xc = x - mu
    var = jnp.mean(xc * xc, axis=-1, keepdims=True)
    return (xc * lax.rsqrt(var + EPS)) * g + b


def _dot(a, b):
    return jnp.dot(a, b, preferred_element_type=F32)


def _dot_nt(a, b, precision=None):
    return lax.dot_general(a, b, (((1,), (1,)), ((), ())), precision=precision,
                           preferred_element_type=F32)


def _mod_kernel(s_ref, w_ref, b_ref, o_ref):
    s = s_ref[...]
    s = s * jax.nn.sigmoid(s)
    o_ref[0, 0] = _dot(s, w_ref[0]) + b_ref[0, 0]


def _modulation(c, c_ctx, w_ada, b_ada):
    depth, d, _ = w_ada.shape
    n_b = c.shape[0]
    rows = -(-(n_b + 1) // 8) * 8
    s = jnp.zeros((rows, d), F32).at[:n_b].set(c).at[n_b].set(c_ctx)
    b3 = b_ada.reshape(depth, 6, 1, d)
    return pl.pallas_call(
        _mod_kernel,
        grid=(depth, 6),
        in_specs=[
            pl.BlockSpec((rows, d), lambda l, n: (0, 0)),
            pl.BlockSpec((1, d, d), lambda l, n: (l, 0, n)),
            pl.BlockSpec((1, 1, 1, d), lambda l, n: (l, n, 0, 0)),
        ],
        out_specs=pl.BlockSpec((1, 1, rows, d), lambda l, n: (l, n, 0, 0)),
        out_shape=jax.ShapeDtypeStruct((depth, 6, rows, d), F32),
        compiler_params=_cparams(("parallel", "parallel")),
        name="adaln_mod",
    )(s, w_ada, b3)


def _sincos_1d(pos, dim):
    half = dim // 2
    omega = 1.0 / (10000.0 ** (jnp.arange(half, dtype=F32) / half))
    ang = pos[:, None] * omega[None, :]
    return jnp.concatenate([jnp.sin(ang), jnp.cos(ang)], axis=-1)


def _grid_pos_embed(rows, dim):
    emb_r = _sincos_1d(jnp.arange(rows, dtype=F32), dim // 2)
    emb_c = _sincos_1d(jnp.arange(GRID_W, dtype=F32), dim // 2)
    pe = jnp.concatenate([
        jnp.broadcast_to(emb_r[:, None, :], (rows, GRID_W, dim // 2)),
        jnp.broadcast_to(emb_c[None, :, :], (rows, GRID_W, dim // 2))], axis=-1)
    return pe.reshape(rows * GRID_W, dim)


def _embed_kernel(x_ref, pe_ref, o_ref):
    o_ref[0] = x_ref[0] + pe_ref[...]


def _embed(x, tt):
    n_b, n_l, d = x.shape
    pe = _grid_pos_embed(n_l // GRID_W, d)
    return pl.pallas_call(
        _embed_kernel,
        grid=(n_b, n_l // tt),
        in_specs=[pl.BlockSpec((1, tt, d), lambda b, j: (b, j, 0)),
                  pl.BlockSpec((tt, d), lambda b, j: (j, 0))],
        out_specs=pl.BlockSpec((1, tt, d), lambda b, j: (b, j, 0)),
        out_shape=jax.ShapeDtypeStruct(x.shape, x.dtype),
        compiler_params=_cparams(("parallel", "parallel")),
        name="pos_embed",
    )(x, pe)


def _zs_kernel(x_ref, mod_ref, g_ref, w_ref, b_ref, o_ref):
    h = _prenorm(x_ref[0], g_ref[0:1], mod_ref[1, 0], mod_ref[0, 0])
    o_ref[0] = _dot(h.astype(BF16), w_ref[...]) + b_ref[...]


def _mod_spec(d, row_of_batch):
    return pl.BlockSpec((6, 1, 1, d), lambda b, j: (0, row_of_batch(b), 0, 0))


def _zs_pass(x, mod4, row_of_batch, norm_g, w_zs, b_zs, tt):
    n_b, n_l, d = x.shape
    return pl.pallas_call(
        _zs_kernel,
        grid=(n_b, n_l // tt),
        in_specs=[pl.BlockSpec((1, tt, d), lambda b, j: (b, j, 0)),
                  _mod_spec(d, row_of_batch),
                  _const_spec(norm_g.shape), _const_spec(w_zs.shape), _const_spec(b_zs.shape)],
        out_specs=pl.BlockSpec((1, tt, W_BR), lambda b, j: (b, j, 0)),
        out_shape=jax.ShapeDtypeStruct((n_b, n_l, W_BR), F32),
        compiler_params=_cparams(("parallel", "parallel")),
        name="ssm_in_proj",
    )(x, mod4, norm_g, w_zs, b_zs)


def _ssm_compact_params(lam_re, lam_im, log_dt, b_re, b_im, c_re, c_im):
    depth, _, n_g, n_p = lam_re.shape
    n_slab = n_g // SLAB_GROUPS

    def vec(v):
        return v.reshape(depth, 2, n_slab, SLAB_STATE).transpose(0, 2, 1, 3)

    lam = jnp.stack([vec(lam_re), vec(lam_im),
                     vec(jnp.broadcast_to(log_dt[..., None], lam_re.shape))], axis=3)

    def mat(re, im, chan_axis_last):
        t = jnp.stack([re, im], axis=2)
        t = t.reshape(depth, 2, 2, n_slab, SLAB_GROUPS, *t.shape[4:])
        if chan_axis_last:
            t = t.transpose(0, 3, 1, 2, 6, 4, 5)
        else:
            t = t.transpose(0, 3, 1, 2, 5, 4, 6)
        return t.reshape(depth, n_slab, 2, 2, SSM_GROUP, SLAB_STATE)

    return lam, mat(b_re, b_im, True), mat(c_re, c_im, False)


def _build_chunk_matrices(lam_ref, bt_ref, ct_ref, msum_scr, wp_scr, nt_scr, aq_scr):
    q = SSM_Q
    rowg = lax.broadcasted_iota(jnp.int32, (LANES, SLAB_STATE), 0) // SSM_GROUP
    colg = lax.broadcasted_iota(jnp.int32, (LANES, SLAB_STATE), 1) // SSM_STATE
    same_group = rowg == colg

    def block_diag(v):
        return jnp.where(same_group, jnp.concatenate([v] * SLAB_GROUPS, axis=0), 0.0)

    kk = []
    for d in range(2):
        lam_re, lam_im = lam_ref[0, d, 0:1, :], lam_ref[0, d, 1:2, :]
        dt = jnp.exp(lam_ref[0, d, 2:3, :])
        ang = lam_im * dt
        a_mag = jnp.exp(lam_re * dt)
        a_re, a_im = a_mag * jnp.cos(ang), a_mag * jnp.sin(ang)
        den = lam_re * lam_re + lam_im * lam_im
        f_re = ((a_re - 1.0) * lam_re + a_im * lam_im) / den
        f_im = (a_im * lam_re - (a_re - 1.0) * lam_im) / den
        bt_re, bt_im = bt_ref[0, d, 0], bt_ref[0, d, 1]
        bb_re = f_re * bt_re - f_im * bt_im
        bb_im = f_re * bt_im + f_im * bt_re
        ct_re, ct_im = ct_ref[0, d, 0], ct_ref[0, d, 1]
        pw = []
        for tau in range(q + 1):
            mag = jnp.exp(float(tau) * (lam_re * dt))
            pw.append((mag * jnp.cos(float(tau) * ang), mag * jnp.sin(float(tau) * ang)))
        cps = []
        for tau in range(q + 1):
            pr, pi = pw[tau]
            cps.append(jnp.concatenate([block_diag(ct_re * pr - ct_im * pi),
                                        block_diag(-(ct_re * pi + ct_im * pr))], axis=1).astype(BF16))
        bbd = jnp.concatenate([block_diag(bb_re), block_diag(bb_im)], axis=1).astype(BF16)
        kk.append([_dot_nt(bbd, cps[tau]) for tau in range(q)])
        lo = d * 2 * SLAB_STATE
        for r in range(q):
            pr, pi = pw[q - 1 - r] if d == 0 else pw[r]
            rows = slice(r * LANES, (r + 1) * LANES)
            wp_scr[rows, lo:lo + SLAB_STATE] = block_diag(pr * bb_re - pi * bb_im).astype(BF16)
            wp_scr[rows, lo + SLAB_STATE:lo + 2 * SLAB_STATE] = block_diag(pr * bb_im + pi * bb_re).astype(BF16)
            nt_scr[rows, lo:lo + 2 * SLAB_STATE] = cps[r + 1] if d == 0 else cps[q - r]
        nsb = SLAB_STATE // LANES
        for part in range(2):
            for row in range(SUBLANES):
                sb = row % nsb
                aq_scr[2 * d + part, row:row + 1, :] = pw[q][part][:, sb * LANES:(sb + 1) * LANES]
    for r in range(q):
        for s in range(q):
            if s > r:
                blk = kk[0][s - r]
            elif s < r:
                blk = kk[1][r - s]
            else:
                blk = kk[0][0] + kk[1][0]
            msum_scr[r * LANES:(r + 1) * LANES, s * LANES:(s + 1) * LANES] = blk.astype(BF16)


def _ssm_kernel(zc_ref, zx_ref, lam_ref, bt_ref, ct_ref, yc_ref, yx_ref,
                u_scr, s_scr, msum_scr, wp_scr, nt_scr, aq_scr, *, ncc, ncx):
    nc = ncc + ncx
    q = SSM_Q
    nsb = SLAB_STATE // LANES
    nb = SUBLANES // nsb

    @pl.when(pl.program_id(1) == 0)
    def _():
        _build_chunk_matrices(lam_ref, bt_ref, ct_ref, msum_scr, wp_scr, nt_scr, aq_scr)

    for bb in range(nb):
        for k in range(q):
            lanes = slice(k * LANES, (k + 1) * LANES)
            u_scr[bb * nc:bb * nc + ncc, lanes] = zc_ref[bb, pl.ds(k, ncc, stride=q), :].astype(BF16)
            u_scr[bb * nc + ncc:(bb + 1) * nc, lanes] = zx_ref[bb, pl.ds(k, ncx, stride=q), :].astype(BF16)

    def tile_rows(bb, sb):
        return pl.ds(bb * nsb + sb, nc, stride=SUBLANES)

    for bb in range(nb):
        s = _dot(u_scr[bb * nc:(bb + 1) * nc, :], wp_scr[...])
        for dp in range(4):
            for sb in range(nsb):
                blk = dp * nsb + sb
                s_scr[dp, tile_rows(bb, sb), :] = s[:, blk * LANES:(blk + 1) * LANES]

    a = [aq_scr[dp] for dp in range(4)]

    def step(t, carry):
        hfr, hfi, hbr, hbi = carry
        row_b = jnp.where(t < ncc, ncc - 1 - t, nc + ncc - 1 - t)
        rf = pl.ds(pl.multiple_of(t * SUBLANES, SUBLANES), SUBLANES)
        rb = pl.ds(pl.multiple_of(row_b * SUBLANES, SUBLANES), SUBLANES)
        sfr, sfi, sbr, sbi = s_scr[0, rf, :], s_scr[1, rf, :], s_scr[2, rb, :], s_scr[3, rb, :]
        s_scr[0, rf, :] = hfr
        s_scr[1, rf, :] = hfi
        s_scr[2, rb, :] = hbr
        s_scr[3, rb, :] = hbi
        return (a[0] * hfr - a[1] * hfi + sfr, a[0] * hfi + a[1] * hfr + sfi,
                a[2] * hbr - a[3] * hbi + sbr, a[2] * hbi + a[3] * hbr + sbi)

    h0 = jnp.zeros((SUBLANES, LANES), F32)
    lax.fori_loop(0, nc, step, (h0, h0, h0, h0), unroll=8)

    for bb in range(nb):
        h_in = jnp.concatenate([s_scr[dp, tile_rows(bb, sb), :] for dp in range(4) for sb in range(nsb)],
                               axis=-1).astype(BF16)
        y = _dot(u_scr[bb * nc:(bb + 1) * nc, :], msum_scr[...]) + _dot_nt(h_in, nt_scr[...])
        for k in range(q):
            lanes = slice(k * LANES, (k + 1) * LANES)
            yc_ref[bb, pl.ds(k, ncc, stride=q), :] = y[0:ncc, lanes]
            yx_ref[bb, pl.ds(k, ncx, stride=q), :] = y[ncc:nc, lanes]


def _ssm_scan(zs_c, zs_x, lam, b_t, c_t):
    n_b, ctx_len, w = zs_c.shape
    n_l = zs_x.shape[1]
    n_slab = w // LANES
    nb = SUBLANES // (SLAB_STATE // LANES)
    ncc, ncx = ctx_len // SSM_Q, n_l // SSM_Q
    nc = ncc + ncx
    assert nc % BF16_ROWS == 0 and ncc % BF16_ROWS == 0 and n_b % nb == 0
    par_spec = pl.BlockSpec((1, 2, 2, SSM_GROUP, SLAB_STATE), lambda k, g: (k, 0, 0, 0, 0))
    return pl.pallas_call(
        functools.partial(_ssm_kernel, ncc=ncc, ncx=ncx),
        grid=(n_slab, n_b // nb),
        in_specs=[pl.BlockSpec((nb, ctx_len, LANES), lambda k, g: (g, 0, k)),
                  pl.BlockSpec((nb, n_l, LANES), lambda k, g: (g, 0, k)),
                  pl.BlockSpec((1, 2, 3, SLAB_STATE), lambda k, g: (k, 0, 0, 0)),
                  par_spec, par_spec],
        out_specs=[pl.BlockSpec((nb, ctx_len, LANES), lambda k, g: (g, 0, k)),
                   pl.BlockSpec((nb, n_l, LANES), lambda k, g: (g, 0, k))],
        out_shape=[jax.ShapeDtypeStruct(zs_c.shape, F32), jax.ShapeDtypeStruct(zs_x.shape, F32)],
        scratch_shapes=[pltpu.VMEM((nb * nc, FLAT), BF16),
                        pltpu.VMEM((4, nc * SUBLANES, LANES), F32),
                        pltpu.VMEM((FLAT, FLAT), BF16),
                        pltpu.VMEM((FLAT, 4 * SLAB_STATE), BF16),
                        pltpu.VMEM((FLAT, 4 * SLAB_STATE), BF16),
                        pltpu.VMEM((4, SUBLANES, LANES), F32)],
        compiler_params=_cparams(("arbitrary", "arbitrary")),
        name="s5_scan",
    )(zs_c, zs_x, lam, b_t, c_t)


def _mixer_kernel(xc_ref, xp_ref, xn_ref, ys_ref, mod_ref, g_ref, w_in_ref, b_in_ref, cw_ref, v512_ref,
                  cwo_ref, sw_ref, sbias_ref, swo_ref, wglu_ref, wso_ref, wo_ref, v1024_ref,
                  o_ref, hext, aext, ash, cout, *, tt, nt):
    j = pl.program_id(1)
    w = W_BR
    d = xc_ref.shape[-1]
    x = xc_ref[0]
    g0 = g_ref[0:1]
    shift, scale, gate = mod_ref[0, 0], mod_ref[1, 0], mod_ref[2, 0]

    def pre(v):
        return _prenorm(v, g0, scale, shift).astype(BF16)

    hext[0:HALO] = pre(xp_ref[0])
    hext[HALO:HALO + tt] = pre(x)
    hext[HALO + tt:] = pre(xn_ref[0])
    h = hext[HALO:HALO + tt]

    def branch_gate(idx):
        lo = 5 * w + idx * d
        return jax.nn.sigmoid(_dot(h, w_in_ref[:, lo:lo + d]) + b_in_ref[:, lo:lo + d])

    za = _dot(hext[...], w_in_ref[:, 0:2 * w]) + b_in_ref[:, 0:2 * w]
    a = za[:, :w] * jax.nn.sigmoid(za[:, w:])
    row = lax.broadcasted_iota(jnp.int32, (tt + 2 * HALO, 1), 0)
    has_prev = (j > 0).astype(F32)
    has_next = (j < nt - 1).astype(F32)
    aext[...] = a * jnp.where(row < HALO, has_prev, jnp.where(row >= HALO + tt, has_next, 1.0))
    for r in range(1, SUBLANES):
        ash[r - 1] = aext[pl.ds(r, tt + 2 * HALO - SUBLANES), :]

    def conv_rows(i, carry):
        base = pl.multiple_of(i * CONV_ROWS, CONV_ROWS)
        acc = jnp.broadcast_to(v512_ref[0:1], (CONV_ROWS, w))
        for k in range(CONV_K):
            qo, r = divmod(k + HALO - CONV_K // 2, SUBLANES)
            rows = pl.ds(base + qo * SUBLANES, CONV_ROWS)
            src = aext[rows, :] if r == 0 else ash[r - 1, rows, :]
            acc = acc + jnp.concatenate([cw_ref[k]] * (CONV_ROWS // SUBLANES), axis=0) * src
        cout[pl.ds(base, CONV_ROWS), :] = acc
        return carry

    lax.fori_loop(0, tt // CONV_ROWS, conv_rows, 0)
    a2 = _layernorm(cout[...], v512_ref[1:2], v512_ref[2:3])
    a2 = a2 * jax.nn.sigmoid(a2)
    o_a = _dot(a2.astype(BF16), cwo_ref[...]) + v1024_ref[0:1]
    merged = branch_gate(0) * o_a

    zb = jax.nn.gelu(_dot(h, w_in_ref[:, 2 * w:4 * w]) + b_in_ref[:, 2 * w:4 * w])
    u_b = zb[:, :w]
    v_b = _layernorm(zb[:, w:], v512_ref[3:4], v512_ref[4:5]).astype(BF16)
    lane = lax.broadcasted_iota(jnp.int32, (CHUNK, LANES), 1)
    lo = lane < (w // SGU_HEADS)
    zero = jnp.zeros((CHUNK, LANES), BF16)
    row_blocks = []
    for c in range(tt // CHUNK):
        cols = []
        for hp in range(w // LANES):
            vp = v_b[c * CHUNK:(c + 1) * CHUNK, hp * LANES:(hp + 1) * LANES]
            cols.append(_dot(sw_ref[2 * hp], jnp.where(lo, vp, zero))
                        + _dot(sw_ref[2 * hp + 1], jnp.where(lo, zero, vp))
                        + sbias_ref[:, hp * LANES:(hp + 1) * LANES])
        row_blocks.append(jnp.concatenate(cols, axis=1))
    sp = jnp.concatenate(row_blocks, axis=0)
    o_b = _dot((u_b * sp).astype(BF16), swo_ref[...]) + v1024_ref[1:2]
    merged = merged + branch_gate(1) * o_b

    zs = _dot(h, w_in_ref[:, 4 * w:5 * w]) + b_in_ref[:, 4 * w:5 * w]
    y = jax.nn.gelu(ys_ref[0] + v512_ref[5:6] * zs)
    y = y * jax.nn.sigmoid(_dot(y.astype(BF16), wglu_ref[...]) + v512_ref[6:7])
    o_c = _dot(y.astype(BF16), wso_ref[...]) + v1024_ref[2:3]
    merged = merged + branch_gate(2) * o_c

    out = _dot(merged.astype(BF16), wo_ref[...]) + v1024_ref[3:4]
    o_ref[0] = x + gate * (_rms(out) * g_ref[1:2])


def _mixer_pass(x, ysum, mod4, row_of_batch, wts, tt):
    n_b, n_l, d = x.shape
    nt = n_l // tt
    hb = tt // HALO
    last_halo = n_l // HALO - 1
    in_specs = [
        pl.BlockSpec((1, tt, d), lambda b, j: (b, j, 0)),
        pl.BlockSpec((1, HALO, d), lambda b, j: (b, jnp.maximum(j * hb - 1, 0), 0)),
        pl.BlockSpec((1, HALO, d), lambda b, j: (b, jnp.minimum((j + 1) * hb, last_halo), 0)),
        pl.BlockSpec((1, tt, W_BR), lambda b, j: (b, j, 0)),
        _mod_spec(d, row_of_batch),
    ] + [_const_spec(w.shape) for w in wts]
    return pl.pallas_call(
        functools.partial(_mixer_kernel, tt=tt, nt=nt),
        grid=(n_b, nt),
        in_specs=in_specs,
        out_specs=pl.BlockSpec((1, tt, d), lambda b, j: (b, j, 0)),
        out_shape=jax.ShapeDtypeStruct(x.shape, x.dtype),
        scratch_shapes=[pltpu.VMEM((tt + 2 * HALO, d), BF16),
                        pltpu.VMEM((tt + 2 * HALO, W_BR), F32),
                        pltpu.VMEM((SUBLANES - 1, tt + 2 * HALO - SUBLANES, W_BR), F32),
                        pltpu.VMEM((tt, W_BR), F32)],
        compiler_params=_cparams(("parallel", "parallel")),
        name="mixer",
    )(x, x, x, ysum, mod4, *wts)


def _ffn_kernel(x_ref, mod_ref, g_ref, wup_ref, wdn_ref, o_ref, *, blk):
    x = x_ref[0]
    d_ff = wdn_ref.shape[0]
    h = _prenorm(x, g_ref[2:3], mod_ref[4, 0], mod_ref[3, 0]).astype(BF16)
    acc = jnp.zeros(x.shape, F32)
    for n in range(d_ff // blk):
        gg = _dot(h, wup_ref[:, n * blk:(n + 1) * blk])
        uu = _dot(h, wup_ref[:, d_ff + n * blk:d_ff + (n + 1) * blk])
        act = (gg * jax.nn.sigmoid(gg)) * uu
        acc = acc + _dot(act.astype(BF16), wdn_ref[n * blk:(n + 1) * blk, :])
    o_ref[0] = x + mod_ref[5, 0] * (_rms(acc) * g_ref[3:4])


def _ffn_pass(x, mod4, row_of_batch, norm_g, w_up, w_down, tt):
    n_b, n_l, d = x.shape
    return pl.pallas_call(
        functools.partial(_ffn_kernel, blk=256),
        grid=(n_b, n_l // tt),
        in_specs=[pl.BlockSpec((1, tt, d), lambda b, j: (b, j, 0)),
                  _mod_spec(d, row_of_batch),
                  _const_spec(norm_g.shape), _const_spec(w_up.shape), _const_spec(w_down.shape)],
        out_specs=pl.BlockSpec((1, tt, d), lambda b, j: (b, j, 0)),
        out_shape=jax.ShapeDtypeStruct(x.shape, x.dtype),
        compiler_params=_cparams(("parallel", "parallel")),
        name="ffn",
    )(x, mod4, norm_g, w_up, w_down)


def kernel(x, c, ctx, c_ctx, w_ada, b_ada, norm_g, w_in, b_in, conv_w, conv_b, conv_ln_g, conv_ln_b,
           conv_w_out, conv_b_out, sgu_ln_g, sgu_ln_b, sgu_w, sgu_b, sgu_w_out, sgu_b_out,
           ssm_lam_re, ssm_lam_im, ssm_log_dt, ssm_b_re, ssm_b_im, ssm_c_re, ssm_c_im, ssm_d,
           ssm_w_glu, ssm_b_glu, ssm_w_out, ssm_b_out, w_o, b_o, ffn_w_up, ffn_w_down):
    n_b, n_l, d = x.shape
    ctx_len = ctx.shape[1]
    depth = w_ada.shape[0]
    w = W_BR
    tt_x = 512
    tt_c = 256
    assert n_l % tt_x == 0 and ctx_len % tt_c == 0 and n_l % GRID_W == 0
    assert tt_x % CHUNK == 0 and tt_c % CHUNK == 0 and n_b % 2 == 0

    mod = _modulation(c, c_ctx, w_ada, b_ada)
    x = _embed(x, tt_x)
    lam, b_t, c_t = _ssm_compact_params(ssm_lam_re, ssm_lam_im, ssm_log_dt, ssm_b_re, ssm_b_im,
                                        ssm_c_re, ssm_c_im)
    row_x = lambda b: b
    row_c = lambda b: n_b

    for l in range(depth):
        last = l == depth - 1
        mod4 = mod[l].reshape(6, mod.shape[2], 1, d)
        w_in_l = w_in[l].astype(BF16)
        b_in_l = b_in[l].reshape(1, -1)
        w_zs = w_in_l[:, 4 * w:5 * w]
        b_zs = b_in_l[:, 4 * w:5 * w]
        zeros512 = jnp.zeros((w,), F32)
        v512 = jnp.stack([conv_b[l], conv_ln_g[l], conv_ln_b[l], sgu_ln_g[l], sgu_ln_b[l], ssm_d[l],
                          ssm_b_glu[l], zeros512])
        zeros1024 = jnp.zeros((d,), F32)
        v1024 = jnp.stack([conv_b_out[l], sgu_b_out[l], ssm_b_out[l], b_o[l]] + [zeros1024] * 4)
        sbias = jnp.repeat(sgu_b[l].T, w // SGU_HEADS, axis=1)
        conv_w_rep = jnp.broadcast_to(conv_w[l][:, None, :], (CONV_K, SUBLANES, w))
        wts = [norm_g[l], w_in_l, b_in_l, conv_w_rep, v512, conv_w_out[l].astype(BF16),
               sgu_w[l].astype(BF16), sbias, sgu_w_out[l].astype(BF16), ssm_w_glu[l].astype(BF16),
               ssm_w_out[l].astype(BF16), w_o[l].astype(BF16), v1024]
        w_up_l = ffn_w_up[l].astype(BF16)
        w_dn_l = ffn_w_down[l].astype(BF16)

        zs_c = _zs_pass(ctx, mod4, row_c, norm_g[l], w_zs, b_zs, tt_c)
        zs_x = _zs_pass(x, mod4, row_x, norm_g[l], w_zs, b_zs, tt_x)
        y_c, y_x = _ssm_scan(zs_c, zs_x, lam[l], b_t[l], c_t[l])
        if not last:
            ctx = _mixer_pass(ctx, y_c, mod4, row_c, wts, tt_c)
            ctx = _ffn_pass(ctx, mod4, row_c, norm_g[l], w_up_l, w_dn_l, tt_c)
        x = _mixer_pass(x, y_x, mod4, row_x, wts, tt_x)
        x = _ffn_pass(x, mod4, row_x, norm_g[l], w_up_l, w_dn_l, tt_x)
    return x
```
